```python
import math
import jax, jax.numpy as jnp
from jax import lax
import numpy as np

D_MODEL = 4096
BATCH = 4
SEQ = 2048
DEPTH = 4
DEC_BATCH = 128
DEC_SEQ = 1
PAST_LEN = 8192
PAGE_SIZE = 128

N_EVEN = (DEPTH + 1) // 2
N_ODD = DEPTH // 2
EPS = 1e-6
D_A = D_MODEL // 2
D_B = D_MODEL - D_A
POOL_WINDOWS = (2, 4, 8, 16)
N_POOL_GROUPS = len(POOL_WINDOWS)
POOL_GROUP = D_A // N_POOL_GROUPS
POOL_BUF = max(POOL_WINDOWS) - 1
SSM_GROUP = 16
SSM_GROUPS = D_B // SSM_GROUP
SSM_STATE = 64
DT_MIN = 1e-3
DT_MAX = 1e-1
MLA_HEADS = 64
Q_LORA = 1536
KV_LORA = 512
NOPE_DIM = 128
ROPE_DIM = 64
V_DIM = 128
ROPE_THETA = 10000.0
MLA_SCALE = (NOPE_DIM + ROPE_DIM) ** -0.5
Q_BLOCK = 128
NEG = -1e30
N_MEM = 256
XA_HEADS = 4
XA_HEAD_DIM = 256
XA_SCALE = XA_HEAD_DIM ** -0.5
D_FF = ((8 * D_MODEL // 3 + 255) // 256) * 256

kernel_name = 'hybrid_pool_s5_mla_decoder_step'


def rmsnorm(x, g):
    xf = x.astype(jnp.float32)
    xf = xf * lax.rsqrt(jnp.mean(xf * xf, axis=-1, keepdims=True) + EPS)
    return (xf * g.astype(jnp.float32)).astype(x.dtype)


def softmax_f32(s):
    return jax.nn.softmax(s.astype(jnp.float32), axis=-1)


def swiglu(h, w1, w3, w2):
    return (jax.nn.silu(h @ w1) * (h @ w3)) @ w2


def pool_mix(u, buf, w_pool, pool_scale):
    bsz, n_new, _ = u.shape
    nb = buf.shape[1]
    ext = jnp.concatenate([buf.astype(u.dtype), u], axis=1)
    length = nb + n_new
    e4 = ext.astype(jnp.float32).reshape(bsz, length, N_POOL_GROUPS, POOL_GROUP)
    cs = jnp.concatenate([jnp.zeros_like(e4[:, :1]), jnp.cumsum(e4, axis=1)], axis=1)
    rows = jnp.arange(nb, length)
    win = jnp.asarray(POOL_WINDOWS, dtype=jnp.int32)
    hi = rows[:, None] + 1
    lo = jnp.maximum(hi - win[None, :], 0)
    cnt = (hi - lo).astype(jnp.float32)
    gidx = jnp.arange(N_POOL_GROUPS)[None, :]
    win_sum = cs[:, nb + 1:] - cs[:, lo, gidx, :]
    d = win_sum / cnt[None, :, :, None] - e4[:, nb:]
    out = jnp.einsum('bsgc,gcd->bsgd', d, w_pool.astype(jnp.float32)).reshape(bsz, n_new, D_A)
    out = out * pool_scale.astype(jnp.float32)
    return out.astype(u.dtype), ext[:, length - POOL_BUF:]


def s5_mix(u, s0_re, s0_im, lam_re, lam_im, log_dt, b_re, b_im, c_re, c_im, d_skip, w_glu, b_glu):
    bsz, n, _ = u.shape
    f32 = jnp.float32
    uf = u.astype(f32).reshape(bsz, n, SSM_GROUPS, SSM_GROUP)
    lam = lax.complex(lam_re.astype(f32), lam_im.astype(f32))
    dt = jnp.exp(log_dt.astype(f32))[:, None]
    lam_bar = jnp.exp(lam * dt)
    bmat = lax.complex(b_re.astype(f32), b_im.astype(f32))
    b_bar = ((lam_bar - 1.0) / lam)[..., None] * bmat
    bu = jnp.einsum('bsgc,gpc->bsgp', uf.astype(jnp.complex64), b_bar)
    if s0_re is not None:
        s0 = lax.complex(s0_re.astype(f32), s0_im.astype(f32))
        bu = bu.at[:, 0].add(lam_bar[None] * s0)
    a = jnp.broadcast_to(lam_bar, bu.shape)

    def combine(left, right):
        a_l, b_l = left
        a_r, b_r = right
        return a_r * a_l, a_r * b_l + b_r

    _, states = lax.associative_scan(combine, (a, bu), axis=1)
    cmat = lax.complex(c_re.astype(f32), c_im.astype(f32))
    y = jnp.einsum('bsgp,gcp->bsgc', states, cmat).real
    y = y + d_skip.astype(f32).reshape(SSM_GROUPS, SSM_GROUP) * uf
    z = jax.nn.gelu(y.reshape(bsz, n, D_B))
    out = z * jax.nn.sigmoid(z @ w_glu.astype(f32) + b_glu.astype(f32))
    last = states[:, -1]
    return out.astype(u.dtype), last.real, last.imag


def even_mixer(h, pool_buf, s_re, s_im, w_in, w_pool, pool_scale, lam_re, lam_im, log_dt,
               b_re, b_im, c_re, c_im, d_skip, w_glu, b_glu, w_out):
    z = h @ w_in
    y_a, new_buf = pool_mix(z[..., :D_A], pool_buf, w_pool, pool_scale)
    y_b, new_re, new_im = s5_mix(z[..., D_A:], s_re, s_im, lam_re, lam_im, log_dt,
                                 b_re, b_im, c_re, c_im, d_skip, w_glu, b_glu)
    y = jnp.concatenate([y_a, y_b], axis=-1) @ w_out
    return y, new_buf, new_re, new_im


def rope_tables(pos):
    inv = ROPE_THETA ** (-jnp.arange(0, ROPE_DIM, 2, dtype=jnp.float32) / ROPE_DIM)
    ang = pos[:, None] * inv[None, :]
    return jnp.cos(ang), jnp.sin(ang)


def apply_rope(x, cos, sin):
    xf = x.astype(jnp.float32)
    x1, x2 = xf[..., :ROPE_DIM // 2], xf[..., ROPE_DIM // 2:]
    return jnp.concatenate([x1 * cos - x2 * sin, x1 * sin + x2 * cos], axis=-1).astype(x.dtype)


def mla_project(h, cos, sin, w_dq, q_norm, w_uq, w_dkv, kv_norm, w_kr):
    bsz, n, _ = h.shape
    cq = rmsnorm(h @ w_dq, q_norm)
    q = (cq @ w_uq).reshape(bsz, n, MLA_HEADS, NOPE_DIM + ROPE_DIM)
    q_nope = q[..., :NOPE_DIM]
    q_rope = apply_rope(q[..., NOPE_DIM:], cos[:, None, :], sin[:, None, :])
    c_kv = rmsnorm(h @ w_dkv, kv_norm)
    k_rope = apply_rope(h @ w_kr, cos, sin)
    return q_nope, q_rope, c_kv, k_rope


def mla_prompt(q_nope, q_rope, c_kv, k_rope, w_uk, w_uv, w_o):
    bsz, n = c_kv.shape[:2]
    k_nope = jnp.einsum('btc,chd->bthd', c_kv, w_uk)
    v = jnp.einsum('btc,chd->bthd', c_kv, w_uv)
    nblk = n // Q_BLOCK
    qn_b = jnp.moveaxis(q_nope.reshape(bsz, nblk, Q_BLOCK, MLA_HEADS, NOPE_DIM), 1, 0)
    qr_b = jnp.moveaxis(q_rope.reshape(bsz, nblk, Q_BLOCK, MLA_HEADS, ROPE_DIM), 1, 0)
    kpos = jnp.arange(n)

    def block(args):
        qn, qr, blk = args
        s = jnp.einsum('bqhd,bkhd->bhqk', qn, k_nope) + jnp.einsum('bqhr,bkr->bhqk', qr, k_rope)
        s = s.astype(jnp.float32) * MLA_SCALE
        qpos = blk * Q_BLOCK + jnp.arange(Q_BLOCK)
        s = jnp.where(kpos[None, :] <= qpos[:, None], s, NEG)
        p = softmax_f32(s)
        return jnp.einsum('bhqk,bkhd->bqhd', p.astype(v.dtype), v)

    o = lax.map(block, (qn_b, qr_b, jnp.arange(nblk)))
    o = jnp.moveaxis(o, 0, 1).reshape(bsz, n, MLA_HEADS * V_DIM)
    return o @ w_o


def mla_sample(q_nope, q_rope, c_kv, k_rope, lat_past, kr_past, w_uk, w_uv, w_o):
    bsz, n = c_kv.shape[:2]
    past = lat_past.shape[1]
    q_lat = jnp.einsum('bshd,chd->bshc', q_nope, w_uk)
    s_past = jnp.einsum('bshc,btc->bsht', q_lat, lat_past) + jnp.einsum('bshr,btr->bsht', q_rope, kr_past)
    s_new = jnp.einsum('bshc,btc->bsht', q_lat, c_kv) + jnp.einsum('bshr,btr->bsht', q_rope, k_rope)
    causal = jnp.tril(jnp.ones((n, n), dtype=bool))
    s_new = jnp.where(causal[None, :, None, :], s_new.astype(jnp.float32), NEG)
    s = jnp.concatenate([s_past.astype(jnp.float32), s_new], axis=-1) * MLA_SCALE
    p = softmax_f32(s)
    o_lat = (jnp.einsum('bsht,btc->bshc', p[..., :past].astype(lat_past.dtype), lat_past)
             + jnp.einsum('bsht,btc->bshc', p[..., past:].astype(c_kv.dtype), c_kv))
    o = jnp.einsum('bshc,chd->bshd', o_lat, w_uv).reshape(bsz, n, MLA_HEADS * V_DIM)
    return o @ w_o


def mem_kv(mem, w_mk, w_mv):
    bsz = mem.shape[0]
    k = (mem @ w_mk).reshape(bsz, N_MEM, XA_HEADS, XA_HEAD_DIM)
    v = (mem @ w_mv).reshape(bsz, N_MEM, XA_HEADS, XA_HEAD_DIM)
    return k, v


def cross_attn(h, mk, mv, w_xq, w_xo):
    bsz, n, _ = h.shape
    q = (h @ w_xq).reshape(bsz, n, XA_HEADS, XA_HEAD_DIM)
    s = jnp.einsum('bshd,bmhd->bhsm', q, mk).astype(jnp.float32) * XA_SCALE
    p = softmax_f32(s)
    o = jnp.einsum('bhsm,bmhd->bshd', p.astype(mv.dtype), mv).reshape(bsz, n, XA_HEADS * XA_HEAD_DIM)
    return o @ w_xo


def setup_inputs(seed: int = 0) -> dict:
    key = jax.random.key(seed)
    keys = iter(jax.random.split(key, 64))
    f32 = jnp.float32

    def nrm(shape, scale=1.0):
        return jax.random.normal(next(keys), shape, dtype=f32) * scale

    def gain(shape):
        return 1.0 + 0.01 * jax.random.normal(next(keys), shape, dtype=f32)

    n_pages = PAST_LEN // PAGE_SIZE
    used = DEC_BATCH * n_pages
    n_pool = used + used // 4
    page_table = jax.random.permutation(next(keys), n_pool)[:used].reshape(DEC_BATCH, n_pages).astype(jnp.int32)
    xa_w = XA_HEADS * XA_HEAD_DIM
    return {
        'x_prompt': nrm((BATCH, SEQ, D_MODEL)),
        'x_sample': nrm((DEC_BATCH, DEC_SEQ, D_MODEL)),
        'mem_prompt': nrm((BATCH, N_MEM, D_MODEL)),
        'cache_latent': nrm((N_ODD, n_pool, PAGE_SIZE, KV_LORA)),
        'cache_k_rope': nrm((N_ODD, n_pool, PAGE_SIZE, ROPE_DIM)),
        'cache_mem_k': nrm((DEPTH, DEC_BATCH, N_MEM, XA_HEADS, XA_HEAD_DIM)),
        'cache_mem_v': nrm((DEPTH, DEC_BATCH, N_MEM, XA_HEADS, XA_HEAD_DIM)),
        'state_pool': nrm((N_EVEN, DEC_BATCH, POOL_BUF, D_A)),
        'state_ssm_re': nrm((N_EVEN, DEC_BATCH, SSM_GROUPS, SSM_STATE), 0.1),
        'state_ssm_im': nrm((N_EVEN, DEC_BATCH, SSM_GROUPS, SSM_STATE), 0.1),
        'page_table': page_table,
        'norm_mix': gain((DEPTH, D_MODEL)),
        'w_in_ab': nrm((N_EVEN, D_MODEL, D_MODEL), D_MODEL ** -0.5),
        'w_pool': nrm((N_EVEN, N_POOL_GROUPS, POOL_GROUP, POOL_GROUP), POOL_GROUP ** -0.5),
        'pool_scale': gain((N_EVEN, D_A)),
        'lam_re': -0.5 + 0.01 * nrm((N_EVEN, SSM_GROUPS, SSM_STATE)),
        'lam_im': jnp.pi * jnp.arange(SSM_STATE, dtype=f32) + 0.01 * nrm((N_EVEN, SSM_GROUPS, SSM_STATE)),
        'log_dt': jax.random.uniform(next(keys), (N_EVEN, SSM_GROUPS), dtype=f32,
                                     minval=math.log(DT_MIN), maxval=math.log(DT_MAX)),
        'b_re': nrm((N_EVEN, SSM_GROUPS, SSM_STATE, SSM_GROUP), (2.0 * SSM_GROUP) ** -0.5),
        'b_im': nrm((N_EVEN, SSM_GROUPS, SSM_STATE, SSM_GROUP), (2.0 * SSM_GROUP) ** -0.5),
        'c_re': nrm((N_EVEN, SSM_GROUPS, SSM_GROUP, SSM_STATE), (2.0 * SSM_STATE) ** -0.5),
        'c_im': nrm((N_EVEN, SSM_GROUPS, SSM_GROUP, SSM_STATE), (2.0 * SSM_STATE) ** -0.5),
        'd_skip': nrm((N_EVEN, D_B)),
        'w_glu': nrm((N_EVEN, D_B, D_B), D_B ** -0.5),
        'b_glu': nrm((N_EVEN, D_B), 0.01),
        'w_out_ab': nrm((N_EVEN, D_MODEL, D_MODEL), D_MODEL ** -0.5),
        'w_dq': nrm((N_ODD, D_MODEL, Q_LORA), D_MODEL ** -0.5),
        'q_norm': gain((N_ODD, Q_LORA)),
        'w_uq': nrm((N_ODD, Q_LORA, MLA_HEADS * (NOPE_DIM + ROPE_DIM)), Q_LORA ** -0.5),
        'w_dkv': nrm((N_ODD, D_MODEL, KV_LORA), D_MODEL ** -0.5),
        'kv_norm': gain((N_ODD, KV_LORA)),
        'w_kr': nrm((N_ODD, D_MODEL, ROPE_DIM), D_MODEL ** -0.5),
        'w_uk': nrm((N_ODD, KV_LORA, MLA_HEADS, NOPE_DIM), KV_LORA ** -0.5),
        'w_uv': nrm((N_ODD, KV_LORA, MLA_HEADS, V_DIM), KV_LORA ** -0.5),
        'w_o': nrm((N_ODD, MLA_HEADS * V_DIM, D_MODEL), (MLA_HEADS * V_DIM) ** -0.5),
        'norm_xa': gain((DEPTH, D_MODEL)),
        'w_xq': nrm((DEPTH, D_MODEL, xa_w), D_MODEL ** -0.5),
        'w_mk': nrm((DEPTH, D_MODEL, xa_w), D_MODEL ** -0.5),
        'w_mv': nrm((DEPTH, D_MODEL, xa_w), D_MODEL ** -0.5),
        'w_xo': nrm((DEPTH, xa_w, D_MODEL), xa_w ** -0.5),
        'norm_ffn': gain((DEPTH, D_MODEL)),
        'w1': nrm((DEPTH, D_MODEL, D_FF), D_MODEL ** -0.5),
        'w3': nrm((DEPTH, D_MODEL, D_FF), D_MODEL ** -0.5),
        'w2': nrm((DEPTH, D_FF, D_MODEL), D_FF ** -0.5),
        'norm_final': gain((D_MODEL,)),
    }


def reference(x_prompt, x_sample, mem_prompt, cache_latent, cache_k_rope, cache_mem_k, cache_mem_v,
              state_pool, state_ssm_re, state_ssm_im, page_table,
              norm_mix, w_in_ab, w_pool, pool_scale, lam_re, lam_im, log_dt, b_re, b_im, c_re, c_im,
              d_skip, w_glu, b_glu, w_out_ab,
              w_dq, q_norm, w_uq, w_dkv, kv_norm, w_kr, w_uk, w_uv, w_o,
              norm_xa, w_xq, w_mk, w_mv, w_xo, norm_ffn, w1, w3, w2, norm_final):
    xp, xs = x_prompt, x_sample
    n_prompt, n_sample = xp.shape[1], xs.shape[1]
    cos_p, sin_p = rope_tables(jnp.arange(n_prompt, dtype=jnp.float32))
    cos_s, sin_s = rope_tables(PAST_LEN + jnp.arange(n_sample, dtype=jnp.float32))
    empty_buf = jnp.zeros((xp.shape[0], 0, D_A), dtype=xp.dtype)
    pool_p, pool_s, re_p, im_p, re_s, im_s = [], [], [], [], [], []
    lat_p, kr_p, lat_s, kr_s, mk_p, mv_p = [], [], [], [], [], []
    for l in range(DEPTH):
        hp = rmsnorm(xp, norm_mix[l])
        hs = rmsnorm(xs, norm_mix[l])
        if l % 2 == 0:
            e = l // 2
            prm = (w_in_ab[e], w_pool[e], pool_scale[e], lam_re[e], lam_im[e], log_dt[e],
                   b_re[e], b_im[e], c_re[e], c_im[e], d_skip[e], w_glu[e], b_glu[e], w_out_ab[e])
            yp, buf_p, sr_p, si_p = even_mixer(hp, empty_buf, None, None, *prm)
            ys, buf_s, sr_s, si_s = even_mixer(hs, state_pool[e], state_ssm_re[e], state_ssm_im[e], *prm)
            pool_p.append(buf_p)
            pool_s.append(buf_s)
            re_p.append(sr_p)
            im_p.append(si_p)
            re_s.append(sr_s)
            im_s.append(si_s)
        else:
            o = l // 2
            proj = (w_dq[o], q_norm[o], w_uq[o], w_dkv[o], kv_norm[o], w_kr[o])
            qn, qr, ckv, kr = mla_project(hp, cos_p, sin_p, *proj)
            yp = mla_prompt(qn, qr, ckv, kr, w_uk[o], w_uv[o], w_o[o])
            lat_p.append(ckv)
            kr_p.append(kr)
            qn, qr, ckv, kr = mla_project(hs, cos_s, sin_s, *proj)
            lat_past = cache_latent[o][page_table].reshape(xs.shape[0], -1, KV_LORA)
            kr_past = cache_k_rope[o][page_table].reshape(xs.shape[0], -1, ROPE_DIM)
            ys = mla_sample(qn, qr, ckv, kr, lat_past, kr_past, w_uk[o], w_uv[o], w_o[o])
            lat_s.append(ckv)
            kr_s.append(kr)
        xp = xp + yp
        xs = xs + ys
        mkp, mvp = mem_kv(mem_prompt, w_mk[l], w_mv[l])
        mk_p.append(mkp)
        mv_p.append(mvp)
        xp = xp + cross_attn(rmsnorm(xp, norm_xa[l]), mkp, mvp, w_xq[l], w_xo[l])
        xs = xs + cross_attn(rmsnorm(xs, norm_xa[l]), cache_mem_k[l], cache_mem_v[l], w_xq[l], w_xo[l])
        xp = xp + swiglu(rmsnorm(xp, norm_ffn[l]), w1[l], w3[l], w2[l])
        xs = xs + swiglu(rmsnorm(xs, norm_ffn[l]), w1[l], w3[l], w2[l])
    y_prompt = rmsnorm(xp, norm_final)
    y_sample = rmsnorm(xs, norm_final)
    return (y_prompt, y_sample,
            jnp.stack(pool_p), jnp.stack(pool_s),
            jnp.stack(re_p), jnp.stack(im_p), jnp.stack(re_s), jnp.stack(im_s),
            jnp.stack(lat_p), jnp.stack(kr_p), jnp.stack(lat_s), jnp.stack(kr_s),
            jnp.stack(mk_p), jnp.stack(mv_p))
```

```python
import functools
import math

import jax
import jax.numpy as jnp
from jax import lax
from jax.experimental import pallas as pl
from jax.experimental.pallas import tpu as pltpu

F32 = jnp.float32
BF16 = jnp.bfloat16

EPS = 1e-6
POOL_WINDOWS = (2, 4, 8, 16)
POOL_BUF = max(POOL_WINDOWS) - 1
SSM_GROUP = 16
SSM_STATE = 64
MLA_HEADS = 64
NOPE_DIM = 128
ROPE_DIM = 64
V_DIM = 128
KV_LORA = 512
ROPE_THETA = 10000.0
MLA_SCALE = (NOPE_DIM + ROPE_DIM) ** -0.5
NEG = -1e30
XA_HEADS = 4
XA_HEAD_DIM = 256
XA_SCALE = XA_HEAD_DIM ** -0.5
PAGE_SIZE = 128

V7X_VMEM_BYTES = 64 * 1024 * 1024
GROUPS_PER_BLOCK = 16
STATE_LANES = GROUPS_PER_BLOCK * SSM_STATE
SUBLANES = 8
DECODE_PAGES_PER_STEP = 8


def _cparams(semantics, vmem_mib):
    assert vmem_mib * 1024 * 1024 < V7X_VMEM_BYTES
    return pltpu.CompilerParams(dimension_semantics=semantics,
                                vmem_limit_bytes=vmem_mib * 1024 * 1024)


def _dot(a, b):
    return jnp.dot(a, b, preferred_element_type=F32)


def _dot_nt(a, b):
    return lax.dot_general(a, b, (((1,), (1,)), ((), ())), preferred_element_type=F32)


def _sigmoid(x):
    return 1.0 / (1.0 + jnp.exp(-x))


def _gelu_tanh(x):
    c = math.sqrt(2.0 / math.pi)
    return 0.5 * x * (1.0 + jnp.tanh(c * (x + 0.044715 * (x * x * x))))


def _row_block(m, target):
    bm = min(m, target)
    assert m % bm == 0
    return bm


def _rmsnorm_kernel(x_ref, g_ref, o_ref):
    x = x_ref[...].astype(F32)
    ms = jnp.mean(x * x, axis=-1, keepdims=True)
    o_ref[...] = (x * lax.rsqrt(ms + EPS) * g_ref[...]).astype(o_ref.dtype)


def rmsnorm(x, g, out_dtype):
    m, d = x.shape
    bm = _row_block(m, 512)
    return pl.pallas_call(
        _rmsnorm_kernel,
        grid=(m // bm,),
        in_specs=[pl.BlockSpec((bm, d), lambda i: (i, 0)),
                  pl.BlockSpec((1, d), lambda i: (0, 0))],
        out_specs=pl.BlockSpec((bm, d), lambda i: (i, 0)),
        out_shape=jax.ShapeDtypeStruct((m, d), out_dtype),
        compiler_params=_cparams(("parallel",), 40),
        name="rmsnorm",
    )(x, g.reshape(1, d))


def _mm_kernel(*refs, nk, has_res):
    if has_res:
        x_ref, w_ref, r_ref, o_ref = refs[:4]
        scratch = refs[4:]
    else:
        x_ref, w_ref, o_ref = refs[:3]
        r_ref = None
        scratch = refs[3:]

    def finish(acc):
        if has_res:
            acc = r_ref[...] + acc
        o_ref[...] = acc.astype(o_ref.dtype)

    if nk == 1:
        finish(_dot(x_ref[...], w_ref[...]))
    else:
        acc_ref, = scratch
        k = pl.program_id(2)

        @pl.when(k == 0)
        def _():
            acc_ref[...] = jnp.zeros_like(acc_ref)

        acc_ref[...] += _dot(x_ref[...], w_ref[...])

        @pl.when(k == nk - 1)
        def _():
            finish(acc_ref[...])


def matmul(x, w, *, out_dtype, bn, bm=1024, bk=None, residual=None, vmem_mib=48):
    m, kdim = x.shape
    n = w.shape[1]
    bm = _row_block(m, bm)
    bn = min(bn, n)
    bk = kdim if bk is None else bk
    assert n % bn == 0 and kdim % bk == 0
    nk = kdim // bk
    in_specs = [pl.BlockSpec((bm, bk), lambda i, j, k: (i, k)),
                pl.BlockSpec((bk, bn), lambda i, j, k: (k, j))]
    args = [x, w]
    if residual is not None:
        in_specs.append(pl.BlockSpec((bm, bn), lambda i, j, k: (i, j)))
        args.append(residual)
    scratch = [pltpu.VMEM((bm, bn), F32)] if nk > 1 else []
    return pl.pallas_call(
        functools.partial(_mm_kernel, nk=nk, has_res=residual is not None),
        grid=(m // bm, n // bn, nk),
        in_specs=in_specs,
        out_specs=pl.BlockSpec((bm, bn), lambda i, j, k: (i, j)),
        out_shape=jax.ShapeDtypeStruct((m, n), out_dtype),
        scratch_shapes=scratch,
        compiler_params=_cparams(("parallel", "parallel", "arbitrary"), vmem_mib),
        name="matmul",
    )(*args)


def _mm2_res_kernel(xa_ref, xb_ref, w_ref, r_ref, o_ref, *, ka):
    acc = _dot(xa_ref[...], w_ref[0:ka, :]) + _dot(xb_ref[...], w_ref[ka:, :])
    o_ref[...] = r_ref[...] + acc


def matmul2_residual(xa, xb, w, residual, *, bn=512, bm=1024):
    m, ka = xa.shape
    kb = xb.shape[1]
    n = w.shape[1]
    bm = _row_block(m, bm)
    return pl.pallas_call(
        functools.partial(_mm2_res_kernel, ka=ka),
        grid=(m // bm, n // bn),
        in_specs=[pl.BlockSpec((bm, ka), lambda i, j: (i, 0)),
                  pl.BlockSpec((bm, kb), lambda i, j: (i, 0)),
                  pl.BlockSpec((ka + kb, bn), lambda i, j: (0, j)),
                  pl.BlockSpec((bm, bn), lambda i, j: (i, j))],
        out_specs=pl.BlockSpec((bm, bn), lambda i, j: (i, j)),
        out_shape=jax.ShapeDtypeStruct((m, n), F32),
        compiler_params=_cparams(("parallel", "parallel"), 48),
        name="matmul2_residual",
    )(xa, xb, w, residual)


def _swiglu_kernel(x_ref, w1_ref, w3_ref, o_ref):
    x = x_ref[...]
    a = _dot(x, w1_ref[...])
    b = _dot(x, w3_ref[...])
    o_ref[...] = (a * _sigmoid(a) * b).astype(o_ref.dtype)


def swiglu_up(x, w1, w3, *, bn=256, bm=1024):
    m, kdim = x.shape
    n = w1.shape[1]
    bm = _row_block(m, bm)
    assert n % bn == 0
    return pl.pallas_call(
        _swiglu_kernel,
        grid=(m // bm, n // bn),
        in_specs=[pl.BlockSpec((bm, kdim), lambda i, j: (i, 0)),
                  pl.BlockSpec((kdim, bn), lambda i, j: (0, j)),
                  pl.BlockSpec((kdim, bn), lambda i, j: (0, j))],
        out_specs=pl.BlockSpec((bm, bn), lambda i, j: (i, j)),
        out_shape=jax.ShapeDtypeStruct((m, n), BF16),
        compiler_params=_cparams(("parallel", "parallel"), 48),
        name="swiglu_up",
    )(x, w1, w3)


def _glu_kernel(x_ref, w_ref, b_ref, z_ref, o_ref):
    gate = _sigmoid(_dot(x_ref[...], w_ref[...]) + b_ref[...])
    o_ref[...] = (z_ref[...].astype(F32) * gate).astype(o_ref.dtype)


def glu(z, w, b, *, bn=512, bm=1024):
    m, kdim = z.shape
    n = w.shape[1]
    bm = _row_block(m, bm)
    return pl.pallas_call(
        _glu_kernel,
        grid=(m // bm, n // bn),
        in_specs=[pl.BlockSpec((bm, kdim), lambda i, j: (i, 0)),
                  pl.BlockSpec((kdim, bn), lambda i, j: (0, j)),
                  pl.BlockSpec((1, bn), lambda i, j: (0, j)),
                  pl.BlockSpec((bm, bn), lambda i, j: (i, j))],
        out_specs=pl.BlockSpec((bm, bn), lambda i, j: (i, j)),
        out_shape=jax.ShapeDtypeStruct((m, n), BF16),
        compiler_params=_cparams(("parallel", "parallel"), 40),
        name="glu",
    )(z, w, b.reshape(1, n), z)


def _mm_rmsnorm_kernel(x_ref, w_ref, g_ref, o_ref):
    acc = _dot(x_ref[...], w_ref[...])
    ms = jnp.mean(acc * acc, axis=-1, keepdims=True)
    o_ref[...] = (acc * lax.rsqrt(ms + EPS) * g_ref[...]).astype(o_ref.dtype)


def matmul_rmsnorm(x, w, g, *, bm=512):
    m, kdim = x.shape
    n = w.shape[1]
    bm = _row_block(m, bm)
    return pl.pallas_call(
        _mm_rmsnorm_kernel,
        grid=(m // bm,),
        in_specs=[pl.BlockSpec((bm, kdim), lambda i: (i, 0)),
                  pl.BlockSpec((kdim, n), lambda i: (0, 0)),
                  pl.BlockSpec((1, n), lambda i: (0, 0))],
        out_specs=pl.BlockSpec((bm, n), lambda i: (i, 0)),
        out_shape=jax.ShapeDtypeStruct((m, n), BF16),
        compiler_params=_cparams(("parallel",), 48),
        name="matmul_rmsnorm",
    )(x, w, g.reshape(1, n))


def _mla_kv_kernel(x_ref, w_ref, g_ref, cos_ref, sin_ref, ckv_ref, ckvb_ref, kr_ref):
    acc = _dot(x_ref[...], w_ref[...])
    c = acc[:, :KV_LORA]
    ms = jnp.mean(c * c, axis=-1, keepdims=True)
    c = c * lax.rsqrt(ms + EPS) * g_ref[...]
    ckv_ref[...] = c
    ckvb_ref[...] = c.astype(BF16)
    kr = acc[:, KV_LORA:KV_LORA + ROPE_DIM]
    kr_swapped = acc[:, KV_LORA + 128:KV_LORA + 128 + ROPE_DIM]
    kr_ref[...] = kr * cos_ref[...] + kr_swapped * sin_ref[...]


def mla_kv(h, w_kvr, kv_norm, cos_t, sin_t, *, rows_per_seq, bm=512):
    m, kdim = h.shape
    n = w_kvr.shape[1]
    bm = _row_block(min(m, rows_per_seq), bm)
    nseq = rows_per_seq // bm
    return pl.pallas_call(
        _mla_kv_kernel,
        grid=(m // bm,),
        in_specs=[pl.BlockSpec((bm, kdim), lambda i: (i, 0)),
                  pl.BlockSpec((kdim, n), lambda i: (0, 0)),
                  pl.BlockSpec((1, KV_LORA), lambda i: (0, 0)),
                  pl.BlockSpec((bm, ROPE_DIM), lambda i: (i % nseq, 0)),
                  pl.BlockSpec((bm, ROPE_DIM), lambda i: (i % nseq, 0))],
        out_specs=[pl.BlockSpec((bm, KV_LORA), lambda i: (i, 0)),
                   pl.BlockSpec((bm, KV_LORA), lambda i: (i, 0)),
                   pl.BlockSpec((bm, ROPE_DIM), lambda i: (i, 0))],
        out_shape=[jax.ShapeDtypeStruct((m, KV_LORA), F32),
                   jax.ShapeDtypeStruct((m, KV_LORA), BF16),
                   jax.ShapeDtypeStruct((m, ROPE_DIM), F32)],
        compiler_params=_cparams(("parallel",), 40),
        name="mla_kv",
    )(h, w_kvr, kv_norm.reshape(1, KV_LORA), cos_t, sin_t)


def _mm_rope_kernel(x_ref, wa_ref, wb_ref, cos_ref, sin_ref, o_ref):
    x = x_ref[...]
    o_ref[...] = (_dot(x, wa_ref[...]) * cos_ref[...]
                  + _dot(x, wb_ref[...]) * sin_ref[...]).astype(o_ref.dtype)


def matmul_rope(x, wa, wb, cos_t, sin_t, *, rows_per_seq, bn=512, bm=1024):
    m, kdim = x.shape
    n = wa.shape[1]
    bm = _row_block(min(m, rows_per_seq), bm)
    nseq = rows_per_seq // bm
    return pl.pallas_call(
        _mm_rope_kernel,
        grid=(m // bm, n // bn),
        in_specs=[pl.BlockSpec((bm, kdim), lambda i, j: (i, 0)),
                  pl.BlockSpec((kdim, bn), lambda i, j: (0, j)),
                  pl.BlockSpec((kdim, bn), lambda i, j: (0, j)),
                  pl.BlockSpec((bm, bn), lambda i, j: (i % nseq, 0)),
                  pl.BlockSpec((bm, bn), lambda i, j: (i % nseq, 0))],
        out_specs=pl.BlockSpec((bm, bn), lambda i, j: (i, j)),
        out_shape=jax.ShapeDtypeStruct((m, n), BF16),
        compiler_params=_cparams(("parallel", "parallel"), 40),
        name="matmul_rope",
    )(x, wa, wb, cos_t, sin_t)


def _mla_flash_kernel(qn_ref, qr_ref, kn_ref, kr_ref, v_ref, o_ref, acc_ref, *, bq):
    qi = pl.program_id(2)
    row = lax.broadcasted_iota(jnp.int32, (bq, bq), 0)
    col = lax.broadcasted_iota(jnp.int32, (bq, bq), 1)
    for hh in range(2):
        q = jnp.concatenate([qn_ref[:, hh * NOPE_DIM:(hh + 1) * NOPE_DIM],
                             qr_ref[:, hh * ROPE_DIM:(hh + 1) * ROPE_DIM]], axis=1)
        acc_ref[...] = jnp.zeros_like(acc_ref)

        def body(j, carry, hh=hh, q=q):
            m_prev, l_prev = carry
            ks = pl.multiple_of(j * bq, bq)
            k = jnp.concatenate([kn_ref[pl.ds(ks, bq), hh * NOPE_DIM:(hh + 1) * NOPE_DIM],
                                 kr_ref[pl.ds(ks, bq), :]], axis=1)
            s = _dot_nt(q, k) * MLA_SCALE
            s = jnp.where(col + (j - qi) * bq <= row, s, NEG)
            m_new = jnp.maximum(m_prev, jnp.max(s, axis=-1, keepdims=True))
            alpha = jnp.exp(m_prev - m_new)
            p = jnp.exp(s - m_new)
            l_new = alpha * l_prev + jnp.sum(p, axis=-1, keepdims=True)
            v = v_ref[pl.ds(ks, bq), hh * V_DIM:(hh + 1) * V_DIM]
            acc_ref[...] = alpha * acc_ref[...] + _dot(p.astype(BF16), v)
            return m_new, l_new

        m0 = jnp.full((bq, 1), NEG, F32)
        l0 = jnp.zeros((bq, 1), F32)
        _, l_fin = lax.fori_loop(0, qi + 1, body, (m0, l0))
        o_ref[:, hh * V_DIM:(hh + 1) * V_DIM] = (acc_ref[...] / l_fin).astype(o_ref.dtype)


def mla_prompt_attention(q_nope, q_rope, k_nope, k_rope, v, *, bsz, seq, bq=512):
    bq = min(bq, seq)
    nq = seq // bq
    m = bsz * seq
    return pl.pallas_call(
        functools.partial(_mla_flash_kernel, bq=bq),
        grid=(bsz, MLA_HEADS // 2, nq),
        in_specs=[pl.BlockSpec((bq, 2 * NOPE_DIM), lambda b, h, i: (b * nq + i, h)),
                  pl.BlockSpec((bq, 2 * ROPE_DIM), lambda b, h, i: (b * nq + i, h)),
                  pl.BlockSpec((seq, 2 * NOPE_DIM), lambda b, h, i: (b, h)),
                  pl.BlockSpec((seq, ROPE_DIM), lambda b, h, i: (b, 0)),
                  pl.BlockSpec((seq, 2 * V_DIM), lambda b, h, i: (b, h))],
        out_specs=pl.BlockSpec((bq, 2 * V_DIM), lambda b, h, i: (b * nq + i, h)),
        out_shape=jax.ShapeDtypeStruct((m, MLA_HEADS * V_DIM), BF16),
        scratch_shapes=[pltpu.VMEM((bq, V_DIM), F32)],
        compiler_params=_cparams(("parallel", "parallel", "arbitrary"), 40),
        name="mla_prompt_attention",
    )(q_nope, q_rope, k_nope, k_rope, v)


def _head_nt_kernel(x_ref, w_ref, o_ref):
    o_ref[...] = _dot_nt(x_ref[...], w_ref[...]).astype(o_ref.dtype)


def absorb_w_uk(q_nope, w_uk2d):
    bsz = q_nope.shape[0]
    return pl.pallas_call(
        _head_nt_kernel,
        grid=(MLA_HEADS,),
        in_specs=[pl.BlockSpec((bsz, NOPE_DIM), lambda h: (0, h)),
                  pl.BlockSpec((KV_LORA, NOPE_DIM), lambda h: (0, h))],
        out_specs=pl.BlockSpec((None, bsz, KV_LORA), lambda h: (h, 0, 0)),
        out_shape=jax.ShapeDtypeStruct((MLA_HEADS, bsz, KV_LORA), BF16),
        compiler_params=_cparams(("parallel",), 16),
        name="absorb_w_uk",
    )(q_nope, w_uk2d)


def _head_nn_kernel(x_ref, w_ref, o_ref):
    o_ref[...] = _dot(x_ref[...], w_ref[...]).astype(o_ref.dtype)


def expand_w_uv(o_lat, w_uv2d):
    bsz = o_lat.shape[1]
    return pl.pallas_call(
        _head_nn_kernel,
        grid=(MLA_HEADS,),
        in_specs=[pl.BlockSpec((None, bsz, KV_LORA), lambda h: (h, 0, 0)),
                  pl.BlockSpec((KV_LORA, V_DIM), lambda h: (0, h))],
        out_specs=pl.BlockSpec((bsz, V_DIM), lambda h: (0, h)),
        out_shape=jax.ShapeDtypeStruct((bsz, MLA_HEADS * V_DIM), BF16),
        compiler_params=_cparams(("parallel",), 16),
        name="expand_w_uv",
    )(o_lat, w_uv2d)


def _mla_decode_kernel(pt_ref, ql_ref, qr_ref, cnew_ref, krnew_ref, *refs, npages, nchunks):
    lat_refs = refs[:npages]
    kr_refs = refs[npages:2 * npages]
    o_ref = refs[2 * npages]
    m_ref, l_ref, acc_ref = refs[2 * npages + 1:]
    c = pl.program_id(1)
    ql = ql_ref[...]
    qr = qr_ref[...]

    @pl.when(c == 0)
    def _():
        cn = cnew_ref[...]
        s_new = (jnp.sum(ql.astype(F32) * cn, axis=-1, keepdims=True)
                 + jnp.sum(qr.astype(F32) * krnew_ref[...], axis=-1, keepdims=True)) * MLA_SCALE
        m_ref[...] = s_new
        l_ref[...] = jnp.ones_like(l_ref)
        acc_ref[...] = jnp.broadcast_to(cn, acc_ref.shape)

    lats = [r[...].astype(BF16) for r in lat_refs]
    s = jnp.concatenate(
        [_dot_nt(ql, lats[i]) + _dot_nt(qr, kr_refs[i][...].astype(BF16)) for i in range(npages)],
        axis=1) * MLA_SCALE
    m_prev = m_ref[...]
    m_new = jnp.maximum(m_prev, jnp.max(s, axis=-1, keepdims=True))
    alpha = jnp.exp(m_prev - m_new)
    p = jnp.exp(s - m_new)
    l_ref[...] = alpha * l_ref[...] + jnp.sum(p, axis=-1, keepdims=True)
    m_ref[...] = m_new
    pb = p.astype(BF16)
    pv = _dot(pb[:, 0:PAGE_SIZE], lats[0])
    for i in range(1, npages):
        pv = pv + _dot(pb[:, i * PAGE_SIZE:(i + 1) * PAGE_SIZE], lats[i])
    acc_ref[...] = alpha * acc_ref[...] + pv

    @pl.when(c == nchunks - 1)
    def _():
        o_ref[...] = (acc_ref[...] / l_ref[...]).astype(o_ref.dtype)


def mla_decode(q_lat, q_rope, c_new, kr_new, cache_latent, cache_k_rope, page_table, layer):
    bsz, pages_per_seq = page_table.shape
    npages = min(DECODE_PAGES_PER_STEP, pages_per_seq)
    assert pages_per_seq % npages == 0
    nchunks = pages_per_seq // npages

    def page_spec(i, width):
        return pl.BlockSpec(
            (None, None, PAGE_SIZE, width),
            lambda b, c, pt: (layer, pt[b * pages_per_seq + c * npages + i], 0, 0))

    in_specs = [pl.BlockSpec((None, MLA_HEADS, KV_LORA), lambda b, c, pt: (b, 0, 0)),
                pl.BlockSpec((None, MLA_HEADS, ROPE_DIM), lambda b, c, pt: (b, 0, 0)),
                pl.BlockSpec((None, 1, KV_LORA), lambda b, c, pt: (b, 0, 0)),
                pl.BlockSpec((None, 1, ROPE_DIM), lambda b, c, pt: (b, 0, 0))]
    in_specs += [page_spec(i, KV_LORA) for i in range(npages)]
    in_specs += [page_spec(i, ROPE_DIM) for i in range(npages)]
    grid_spec = pltpu.PrefetchScalarGridSpec(
        num_scalar_prefetch=1,
        grid=(bsz, nchunks),
        in_specs=in_specs,
        out_specs=pl.BlockSpec((None, MLA_HEADS, KV_LORA), lambda b, c, pt: (b, 0, 0)),
        scratch_shapes=[pltpu.VMEM((MLA_HEADS, 1), F32),
                        pltpu.VMEM((MLA_HEADS, 1), F32),
                        pltpu.VMEM((MLA_HEADS, KV_LORA), F32)])
    return pl.pallas_call(
        functools.partial(_mla_decode_kernel, npages=npages, nchunks=nchunks),
        grid_spec=grid_spec,
        out_shape=jax.ShapeDtypeStruct((bsz, MLA_HEADS, KV_LORA), BF16),
        compiler_params=_cparams(("parallel", "arbitrary"), 32),
        name="mla_decode",
    )(page_table.reshape(-1), q_lat, q_rope, c_new, kr_new,
      *([cache_latent] * npages), *([cache_k_rope] * npages))


def _xattn_prompt_kernel(q_ref, k_ref, v_ref, o_ref):
    s = _dot_nt(q_ref[...], k_ref[...].astype(BF16)) * XA_SCALE
    s = s - jnp.max(s, axis=-1, keepdims=True)
    e = jnp.exp(s)
    p = e / jnp.sum(e, axis=-1, keepdims=True)
    o_ref[...] = _dot(p.astype(BF16), v_ref[...].astype(BF16)).astype(o_ref.dtype)


def cross_attention_prompt(q, mk, mv, *, bsz, seq, n_mem, bq=1024):
    bq = min(bq, seq)
    nq = seq // bq
    return pl.pallas_call(
        _xattn_prompt_kernel,
        grid=(bsz, XA_HEADS, nq),
        in_specs=[pl.BlockSpec((bq, XA_HEAD_DIM), lambda b, h, i: (b * nq + i, h)),
                  pl.BlockSpec((n_mem, XA_HEAD_DIM), lambda b, h, i: (b, h)),
                  pl.BlockSpec((n_mem, XA_HEAD_DIM), lambda b, h, i: (b, h))],
        out_specs=pl.BlockSpec((bq, XA_HEAD_DIM), lambda b, h, i: (b * nq + i, h)),
        out_shape=jax.ShapeDtypeStruct((bsz * seq, XA_HEADS * XA_HEAD_DIM), BF16),
        compiler_params=_cparams(("parallel", "parallel", "parallel"), 32),
        name="cross_attention_prompt",
    )(q, mk, mv)


def _xattn_sample_kernel(q_ref, k_ref, v_ref, o_ref):
    q = q_ref[...].astype(F32)
    prod = k_ref[...] * q
    for h in range(XA_HEADS):
        sl = slice(h * XA_HEAD_DIM, (h + 1) * XA_HEAD_DIM)
        s = jnp.sum(prod[:, :, sl], axis=-1, keepdims=True) * XA_SCALE
        s = s - jnp.max(s, axis=1, keepdims=True)
        e = jnp.exp(s)
        p = e / jnp.sum(e, axis=1, keepdims=True)
        o_ref[:, :, sl] = jnp.sum(p * v_ref[:, :, sl], axis=1, keepdims=True).astype(o_ref.dtype)


def cross_attention_sample(q, cache_k, cache_v, layer, *, bb=4):
    bsz, _, width = q.shape
    n_mem = cache_k.shape[2]
    bb = min(bb, bsz)
    kv_spec = pl.BlockSpec((None, bb, n_mem, width), lambda i: (layer, i, 0, 0))
    return pl.pallas_call(
        _xattn_sample_kernel,
        grid=(bsz // bb,),
        in_specs=[pl.BlockSpec((bb, 1, width), lambda i: (i, 0, 0)), kv_spec, kv_spec],
        out_specs=pl.BlockSpec((bb, 1, width), lambda i: (i, 0, 0)),
        out_shape=jax.ShapeDtypeStruct((bsz, 1, width), BF16),
        compiler_params=_cparams(("parallel",), 40),
        name="cross_attention_sample",
    )(q, cache_k, cache_v)


def _pool_prompt_kernel(z_ref, w_ref, scale_ref, o_ref):
    g = pl.program_id(1)
    window = jnp.left_shift(2, g)
    x = z_ref[...]
    row = lax.broadcasted_iota(jnp.int32, x.shape, 0)
    total = x
    for k in (1, 2, 4, 8):
        shifted = jnp.where(row >= k, pltpu.roll(total, k, axis=0), 0.0)
        total = jnp.where(k < window, total + shifted, total)
    count = jnp.minimum(row + 1, window).astype(F32)
    d = total / count - x
    y = _dot(d.astype(BF16), w_ref[...]) * scale_ref[...]
    o_ref[...] = y.astype(o_ref.dtype)


def pool_prompt(z, w_pool, pool_scale, *, bsz, seq):
    ngroups, gw, _ = w_pool.shape
    d_a = ngroups * gw
    return pl.pallas_call(
        _pool_prompt_kernel,
        grid=(bsz, ngroups),
        in_specs=[pl.BlockSpec((seq, gw), lambda b, g: (b, g)),
                  pl.BlockSpec((None, gw, gw), lambda b, g: (g, 0, 0)),
                  pl.BlockSpec((1, gw), lambda b, g: (0, g))],
        out_specs=pl.BlockSpec((seq, gw), lambda b, g: (b, g)),
        out_shape=jax.ShapeDtypeStruct((bsz * seq, d_a), BF16),
        compiler_params=_cparams(("parallel", "parallel"), 48),
        name="pool_prompt",
    )(z, w_pool, pool_scale.reshape(1, d_a))


def _pool_sample_kernel(buf_ref, u_ref, w_ref, scale_ref, o_ref):
    g = pl.program_id(0)
    window = jnp.left_shift(2, g)
    buf = buf_ref[...]
    u = u_ref[...]
    row = lax.broadcasted_iota(jnp.int32, buf.shape, 1)
    past = jnp.sum(jnp.where(row >= POOL_BUF + 1 - window, buf, 0.0), axis=1)
    d = (past + u) / window.astype(F32) - u
    y = _dot(d.astype(BF16), w_ref[...]) * scale_ref[...]
    o_ref[...] = y.astype(o_ref.dtype)


def pool_sample(state_pool, layer, z, w_pool, pool_scale):
    ngroups, gw, _ = w_pool.shape
    d_a = ngroups * gw
    bsz = z.shape[0]
    return pl.pallas_call(
        _pool_sample_kernel,
        grid=(ngroups,),
        in_specs=[pl.BlockSpec((None, bsz, POOL_BUF, gw), lambda g: (layer, 0, 0, g)),
                  pl.BlockSpec((bsz, gw), lambda g: (0, g)),
                  pl.BlockSpec((None, gw, gw), lambda g: (g, 0, 0)),
                  pl.BlockSpec((1, gw), lambda g: (0, g))],
        out_specs=pl.BlockSpec((bsz, gw), lambda g: (0, g)),
        out_shape=jax.ShapeDtypeStruct((bsz, d_a), BF16),
        compiler_params=_cparams(("parallel",), 32),
        name="pool_sample",
    )(state_pool, z, w_pool, pool_scale.reshape(1, d_a))


def _s5_params_kernel(lre_ref, lim_ref, logdt_ref, bre_ref, bim_ref,
                      lbre_ref, lbim_ref, bbre_ref, bbim_ref):
    lre = lre_ref[...]
    lim = lim_ref[...]
    dt = jnp.exp(logdt_ref[...])
    mag = jnp.exp(lre * dt)
    lbre = mag * jnp.cos(lim * dt)
    lbim = mag * jnp.sin(lim * dt)
    lbre_ref[...] = lbre
    lbim_ref[...] = lbim
    den = lre * lre + lim * lim
    nre = lbre - 1.0
    qre = (nre * lre + lbim * lim) / den
    qim = (lbim * lre - nre * lim) / den
    bre = bre_ref[...]
    bim = bim_ref[...]
    bbre_ref[...] = qre * bre - qim * bim
    bbim_ref[...] = qre * bim + qim * bre


def s5_params(lam_re, lam_im, log_dt, b_re, b_im):
    g, p = lam_re.shape
    c = b_re.shape[-1]
    b_re_t = jnp.swapaxes(b_re, 1, 2)
    b_im_t = jnp.swapaxes(b_im, 1, 2)
    lb_re, lb_im, bb_re, bb_im = pl.pallas_call(
        _s5_params_kernel,
        out_shape=[jax.ShapeDtypeStruct((g, 1, p), F32), jax.ShapeDtypeStruct((g, 1, p), F32),
                   jax.ShapeDtypeStruct((g, c, p), F32), jax.ShapeDtypeStruct((g, c, p), F32)],
        name="s5_params",
    )(lam_re.reshape(g, 1, p), lam_im.reshape(g, 1, p), log_dt.reshape(g, 1, 1), b_re_t, b_im_t)
    return lb_re.reshape(g, p), lb_im.reshape(g, p), bb_re, bb_im


def _s5_block_weights(bb_re, bb_im, c_re, c_im):
    g = bb_re.shape[0]
    nb = g // GROUPS_PER_BLOCK
    eye = jnp.eye(GROUPS_PER_BLOCK, dtype=F32)

    def bside(bb):
        x = bb.reshape(nb, GROUPS_PER_BLOCK, SSM_GROUP, SSM_STATE)
        x = x[:, :, :, None, :] * eye[None, :, None, :, None]
        return x.reshape(nb, GROUPS_PER_BLOCK * SSM_GROUP, STATE_LANES)

    def cside(cc):
        x = jnp.swapaxes(cc, 1, 2).reshape(nb, GROUPS_PER_BLOCK, SSM_STATE, SSM_GROUP)
        x = x[:, :, :, None, :] * eye[None, :, None, :, None]
        return x.reshape(nb, STATE_LANES, GROUPS_PER_BLOCK * SSM_GROUP)

    b_blk = jnp.concatenate([bside(bb_re), bside(bb_im)], axis=2).astype(BF16)
    c_blk = jnp.concatenate([cside(c_re), cside(-c_im)], axis=1).astype(BF16)
    return b_blk, c_blk


def _cmul(ar, ai, br, bi):
    return ar * br - ai * bi, ar * bi + ai * br


def _s5_prompt_kernel(u_ref, bblk_ref, cblk_ref, lbre_ref, lbim_ref, dskip_ref,
                      z_ref, sre_ref, sim_ref, st_ref, coef_ref, *, seq, chunk):
    n = STATE_LANES
    a1r = jnp.broadcast_to(lbre_ref[...], (SUBLANES, n))
    a1i = jnp.broadcast_to(lbim_ref[...], (SUBLANES, n))
    a2r, a2i = _cmul(a1r, a1i, a1r, a1i)
    a4r, a4i = _cmul(a2r, a2i, a2r, a2i)
    row = lax.broadcasted_iota(jnp.int32, (SUBLANES, n), 0)
    for idx, (ar, ai, k) in enumerate(((a1r, a1i, 1), (a2r, a2i, 2), (a4r, a4i, 4))):
        coef_ref[2 * idx] = jnp.where(row >= k, ar, 0.0)
        coef_ref[2 * idx + 1] = jnp.where(row >= k, ai, 0.0)
    pr, pi = a1r, a1i
    accr, acci = a1r, a1i
    for r in range(1, SUBLANES):
        pr, pi = _cmul(pr, pi, a1r, a1i)
        accr = jnp.where(row >= r, pr, accr)
        acci = jnp.where(row >= r, pi, acci)
    coef_ref[6] = accr
    coef_ref[7] = acci

    def chunk_body(ci, carry):
        t0 = pl.multiple_of(ci * chunk, chunk)
        u = u_ref[pl.ds(t0, chunk), :]
        st_ref[...] = _dot(u.astype(BF16), bblk_ref[...])

        def tile_body(t, carry):
            cr, cim = carry
            r0 = pl.multiple_of(t * SUBLANES, SUBLANES)
            xr = st_ref[pl.ds(r0, SUBLANES), 0:n]
            xi = st_ref[pl.ds(r0, SUBLANES), n:2 * n]
            for idx, k in enumerate((1, 2, 4)):
                sr = pltpu.roll(xr, k, axis=0)
                si = pltpu.roll(xi, k, axis=0)
                dr, di = _cmul(coef_ref[2 * idx], coef_ref[2 * idx + 1], sr, si)
                xr = xr + dr
                xi = xi + di
            dr, di = _cmul(coef_ref[6], coef_ref[7],
                           jnp.broadcast_to(cr, (SUBLANES, n)), jnp.broadcast_to(cim, (SUBLANES, n)))
            xr = xr + dr
            xi = xi + di
            st_ref[pl.ds(r0, SUBLANES), 0:n] = xr
            st_ref[pl.ds(r0, SUBLANES), n:2 * n] = xi
            return xr[SUBLANES - 1:SUBLANES, :], xi[SUBLANES - 1:SUBLANES, :]

        carry = lax.fori_loop(0, chunk // SUBLANES, tile_body, carry)
        y = _dot(st_ref[...].astype(BF16), cblk_ref[...]) + dskip_ref[...] * u
        z_ref[pl.ds(t0, chunk), :] = _gelu_tanh(y).astype(z_ref.dtype)
        return carry

    zero = jnp.zeros((1, n), F32)
    cr, cim = lax.fori_loop(0, seq // chunk, chunk_body, (zero, zero))
    sre_ref[...] = cr
    sim_ref[...] = cim


def s5_prompt(z, b_blk, c_blk, lb_re, lb_im, d_skip, *, bsz, seq, col0, chunk=256):
    nb, bw, _ = b_blk.shape
    d_b = nb * bw
    chunk = min(chunk, seq)
    cb0 = col0 // bw
    n = STATE_LANES
    out = pl.pallas_call(
        functools.partial(_s5_prompt_kernel, seq=seq, chunk=chunk),
        grid=(bsz, nb),
        in_specs=[pl.BlockSpec((seq, bw), lambda b, g: (b, cb0 + g)),
                  pl.BlockSpec((None, bw, 2 * n), lambda b, g: (g, 0, 0)),
                  pl.BlockSpec((None, 2 * n, bw), lambda b, g: (g, 0, 0)),
                  pl.BlockSpec((1, n), lambda b, g: (0, g)),
                  pl.BlockSpec((1, n), lambda b, g: (0, g)),
                  pl.BlockSpec((1, bw), lambda b, g: (0, g))],
        out_specs=[pl.BlockSpec((seq, bw), lambda b, g: (b, g)),
                   pl.BlockSpec((None, None, 1, n), lambda b, g: (b, g, 0, 0)),
                   pl.BlockSpec((None, None, 1, n), lambda b, g: (b, g, 0, 0))],
        out_shape=[jax.ShapeDtypeStruct((bsz * seq, d_b), BF16),
                   jax.ShapeDtypeStruct((bsz, nb, 1, n), F32),
                   jax.ShapeDtypeStruct((bsz, nb, 1, n), F32)],
        scratch_shapes=[pltpu.VMEM((chunk, 2 * n), F32),
                        pltpu.VMEM((8, SUBLANES, n), F32)],
        compiler_params=_cparams(("parallel", "parallel"), 32),
        name="s5_prompt",
    )(z, b_blk, c_blk, lb_re.reshape(1, -1), lb_im.reshape(1, -1), d_skip.reshape(1, d_b))
    zact, s_re, s_im = out
    return zact, s_re.reshape(bsz, nb * n), s_im.reshape(bsz, nb * n)


def _s5_sample_kernel(u_ref, s0re_ref, s0im_ref, bblk_ref, cblk_ref, lbre_ref, lbim_ref, dskip_ref,
                      z_ref, sre_ref, sim_ref):
    n = STATE_LANES
    u = u_ref[...]
    bu = _dot(u.astype(BF16), bblk_ref[...])
    dr, di = _cmul(lbre_ref[...], lbim_ref[...], s0re_ref[...], s0im_ref[...])
    sr = dr + bu[:, 0:n]
    si = di + bu[:, n:2 * n]
    sre_ref[...] = sr
    sim_ref[...] = si
    st = jnp.concatenate([sr, si], axis=1).astype(BF16)
    y = _dot(st, cblk_ref[...]) + dskip_ref[...] * u
    z_ref[...] = _gelu_tanh(y).astype(z_ref.dtype)


def s5_sample(z, s0_re, s0_im, layer, b_blk, c_blk, lb_re, lb_im, d_skip, *, col0):
    nb, bw, _ = b_blk.shape
    d_b = nb * bw
    bsz = z.shape[0]
    cb0 = col0 // bw
    n = STATE_LANES
    state_spec = pl.BlockSpec((None, bsz, n), lambda g: (layer, 0, g))
    return pl.pallas_call(
        _s5_sample_kernel,
        grid=(nb,),
        in_specs=[pl.BlockSpec((bsz, bw), lambda g: (0, cb0 + g)),
                  state_spec, state_spec,
                  pl.BlockSpec((None, bw, 2 * n), lambda g: (g, 0, 0)),
                  pl.BlockSpec((None, 2 * n, bw), lambda g: (g, 0, 0)),
                  pl.BlockSpec((1, n), lambda g: (0, g)),
                  pl.BlockSpec((1, n), lambda g: (0, g)),
                  pl.BlockSpec((1, bw), lambda g: (0, g))],
        out_specs=[pl.BlockSpec((bsz, bw), lambda g: (0, g)),
                   pl.BlockSpec((bsz, n), lambda g: (0, g)),
                   pl.BlockSpec((bsz, n), lambda g: (0, g))],
        out_shape=[jax.ShapeDtypeStruct((bsz, d_b), BF16),
                   jax.ShapeDtypeStruct((bsz, nb * n), F32),
                   jax.ShapeDtypeStruct((bsz, nb * n), F32)],
        compiler_params=_cparams(("parallel",), 32),
        name="s5_sample",
    )(z, s0_re, s0_im, b_blk, c_blk, lb_re.reshape(1, -1), lb_im.reshape(1, -1), d_skip.reshape(1, d_b))


def _rope_tables(pos, reps):
    inv = ROPE_THETA ** (-jnp.arange(0, ROPE_DIM, 2, dtype=F32) / ROPE_DIM)
    ang = pos[:, None] * inv[None, :]
    cos, sin = jnp.cos(ang), jnp.sin(ang)
    cos_t = jnp.concatenate([cos, cos], axis=-1)
    sin_t = jnp.concatenate([-sin, sin], axis=-1)
    return jnp.tile(cos_t, (1, reps)), jnp.tile(sin_t, (1, reps))


def _swap_rope_halves(w):
    half = ROPE_DIM // 2
    return jnp.concatenate([w[..., half:], w[..., :half]], axis=-1)


def kernel(x_prompt, x_sample, mem_prompt, cache_latent, cache_k_rope, cache_mem_k, cache_mem_v,
           state_pool, state_ssm_re, state_ssm_im, page_table,
           norm_mix, w_in_ab, w_pool, pool_scale, lam_re, lam_im, log_dt, b_re, b_im, c_re, c_im,
           d_skip, w_glu, b_glu, w_out_ab,
           w_dq, q_norm, w_uq, w_dkv, kv_norm, w_kr, w_uk, w_uv, w_o,
           norm_xa, w_xq, w_mk, w_mv, w_xo, norm_ffn, w1, w3, w2, norm_final):
    bsz, seq, d_model = x_prompt.shape
    dec_b = x_sample.shape[0]
    depth = norm_mix.shape[0]
    n_mem = mem_prompt.shape[1]
    d_a = pool_scale.shape[1]
    xa_w = XA_HEADS * XA_HEAD_DIM
    past_len = page_table.shape[1] * PAGE_SIZE
    groups = lam_re.shape[1]

    xp = x_prompt.reshape(bsz * seq, d_model)
    xs = x_sample.reshape(dec_b, d_model)
    mem_b = mem_prompt.reshape(bsz * n_mem, d_model).astype(BF16)
    cache_mk = cache_mem_k.reshape(depth, dec_b, n_mem, xa_w)
    cache_mv = cache_mem_v.reshape(depth, dec_b, n_mem, xa_w)
    ssm_re0 = state_ssm_re.reshape(-1, dec_b, groups * SSM_STATE)
    ssm_im0 = state_ssm_im.reshape(-1, dec_b, groups * SSM_STATE)

    rope_reps = 512 // ROPE_DIM
    cos_p, sin_p = _rope_tables(jnp.arange(seq, dtype=F32), rope_reps)
    pos_s = jnp.full((dec_b,), past_len, dtype=F32)
    cos_s, sin_s = _rope_tables(pos_s, rope_reps)

    pool_p, pool_s, re_p, im_p, re_s, im_s = [], [], [], [], [], []
    lat_p, kr_p, lat_s, kr_s, mk_p, mv_p = [], [], [], [], [], []

    for l in range(depth):
        hp = rmsnorm(xp, norm_mix[l], BF16)
        hs = rmsnorm(xs, norm_mix[l], BF16)
        if l % 2 == 0:
            e = l // 2
            w_in = w_in_ab[e].astype(BF16)
            w_pool_b = w_pool[e].astype(BF16)
            w_glu_b = w_glu[e].astype(BF16)
            w_out = w_out_ab[e].astype(BF16)
            lb_re, lb_im, bb_re, bb_im = s5_params(lam_re[e], lam_im[e], log_dt[e], b_re[e], b_im[e])
            b_blk, c_blk = _s5_block_weights(bb_re, bb_im, c_re[e], c_im[e])

            zp = matmul(hp, w_in, out_dtype=F32, bn=512)
            ya_p = pool_prompt(zp, w_pool_b, pool_scale[e], bsz=bsz, seq=seq)
            zact_p, sre_p, sim_p = s5_prompt(zp, b_blk, c_blk, lb_re, lb_im, d_skip[e],
                                             bsz=bsz, seq=seq, col0=d_a)
            yb_p = glu(zact_p, w_glu_b, b_glu[e])
            xp = matmul2_residual(ya_p, yb_p, w_out, xp)
            pool_p.append(zp.reshape(bsz, seq, d_model)[:, seq - POOL_BUF:, :d_a])
            re_p.append(sre_p.reshape(bsz, groups, SSM_STATE))
            im_p.append(sim_p.reshape(bsz, groups, SSM_STATE))

            zs = matmul(hs, w_in, out_dtype=F32, bn=512)
            ya_s = pool_sample(state_pool, e, zs, w_pool_b, pool_scale[e])
            zact_s, sre_s, sim_s = s5_sample(zs, ssm_re0, ssm_im0, e, b_blk, c_blk, lb_re, lb_im,
                                             d_skip[e], col0=d_a)
            yb_s = glu(zact_s, w_glu_b, b_glu[e])
            xs = matmul2_residual(ya_s, yb_s, w_out, xs)
            pool_s.append(jnp.concatenate([state_pool[e][:, 1:], zs[:, None, :d_a]], axis=1))
            re_s.append(sre_s.reshape(dec_b, groups, SSM_STATE))
            im_s.append(sim_s.reshape(dec_b, groups, SSM_STATE))
        else:
            o = l // 2
            w_dq_b = w_dq[o].astype(BF16)
            w_uq3 = w_uq[o].reshape(-1, MLA_HEADS, NOPE_DIM + ROPE_DIM)
            w_uq_nope = w_uq3[:, :, :NOPE_DIM].reshape(-1, MLA_HEADS * NOPE_DIM).astype(BF16)
            w_uq_rope3 = w_uq3[:, :, NOPE_DIM:]
            w_uq_rope = w_uq_rope3.reshape(-1, MLA_HEADS * ROPE_DIM).astype(BF16)
            w_uq_rope_sw = _swap_rope_halves(w_uq_rope3).reshape(-1, MLA_HEADS * ROPE_DIM).astype(BF16)
            pad = jnp.zeros((d_model, 128 - ROPE_DIM), F32)
            w_kvr = jnp.concatenate([w_dkv[o], w_kr[o], pad, _swap_rope_halves(w_kr[o]), pad],
                                    axis=1).astype(BF16)
            w_uk2d = w_uk[o].reshape(KV_LORA, MLA_HEADS * NOPE_DIM).astype(BF16)
            w_uv2d = w_uv[o].reshape(KV_LORA, MLA_HEADS * V_DIM).astype(BF16)
            w_o_b = w_o[o].astype(BF16)

            cq_p = matmul_rmsnorm(hp, w_dq_b, q_norm[o])
            qn_p = matmul(cq_p, w_uq_nope, out_dtype=BF16, bn=1024)
            qr_p = matmul_rope(cq_p, w_uq_rope, w_uq_rope_sw, cos_p, sin_p, rows_per_seq=seq)
            ckv_p, ckvb_p, krope_p = mla_kv(hp, w_kvr, kv_norm[o], cos_p[:, :ROPE_DIM],
                                            sin_p[:, :ROPE_DIM], rows_per_seq=seq)
            kn_p = matmul(ckvb_p, w_uk2d, out_dtype=BF16, bn=2048)
            v_p = matmul(ckvb_p, w_uv2d, out_dtype=BF16, bn=2048)
            o_p = mla_prompt_attention(qn_p, qr_p, kn_p, krope_p.astype(BF16), v_p, bsz=bsz, seq=seq)
            xp = matmul(o_p, w_o_b, out_dtype=F32, bn=512, bk=4096, residual=xp)
            lat_p.append(ckv_p.reshape(bsz, seq, KV_LORA))
            kr_p.append(krope_p.reshape(bsz, seq, ROPE_DIM))

            cq_s = matmul_rmsnorm(hs, w_dq_b, q_norm[o])
            qn_s = matmul(cq_s, w_uq_nope, out_dtype=BF16, bn=1024)
            qr_s = matmul_rope(cq_s, w_uq_rope, w_uq_rope_sw, cos_s, sin_s, rows_per_seq=dec_b)
            ckv_s, _, krope_s = mla_kv(hs, w_kvr, kv_norm[o], cos_s[:, :ROPE_DIM],
                                       sin_s[:, :ROPE_DIM], rows_per_seq=dec_b)
            q_lat = jnp.swapaxes(absorb_w_uk(qn_s, w_uk2d), 0, 1)
            o_lat = mla_decode(q_lat, qr_s.reshape(dec_b, MLA_HEADS, ROPE_DIM),
                               ckv_s.reshape(dec_b, 1, KV_LORA), krope_s.reshape(dec_b, 1, ROPE_DIM),
                               cache_latent, cache_k_rope, page_table, o)
            o_s = expand_w_uv(jnp.swapaxes(o_lat, 0, 1), w_uv2d)
            xs = matmul(o_s, w_o_b, out_dtype=F32, bn=512, bk=4096, residual=xs)
            lat_s.append(ckv_s.reshape(dec_b, 1, KV_LORA))
            kr_s.append(krope_s.reshape(dec_b, 1, ROPE_DIM))

        w_xq_b = w_xq[l].astype(BF16)
        w_xo_b = w_xo[l].astype(BF16)
        mk = matmul(mem_b, w_mk[l].astype(BF16), out_dtype=F32, bn=512)
        mv = matmul(mem_b, w_mv[l].astype(BF16), out_dtype=F32, bn=512)
        mk_p.append(mk.reshape(bsz, n_mem, XA_HEADS, XA_HEAD_DIM))
        mv_p.append(mv.reshape(bsz, n_mem, XA_HEADS, XA_HEAD_DIM))
        qx_p = matmul(rmsnorm(xp, norm_xa[l], BF16), w_xq_b, out_dtype=BF16, bn=512)
        ox_p = cross_attention_prompt(qx_p, mk, mv, bsz=bsz, seq=seq, n_mem=n_mem)
        xp = matmul(ox_p, w_xo_b, out_dtype=F32, bn=512, residual=xp)
        qx_s = matmul(rmsnorm(xs, norm_xa[l], BF16), w_xq_b, out_dtype=BF16, bn=512)
        ox_s = cross_attention_sample(qx_s.reshape(dec_b, 1, xa_w), cache_mk, cache_mv, l)
        xs = matmul(ox_s.reshape(dec_b, xa_w), w_xo_b, out_dtype=F32, bn=512, residual=xs)

        w1_b = w1[l].astype(BF16)
        w3_b = w3[l].astype(BF16)
        w2_b = w2[l].astype(BF16)
        d_ff = w1_b.shape[1]
        g_p = swiglu_up(rmsnorm(xp, norm_ffn[l], BF16), w1_b, w3_b)
        xp = matmul(g_p, w2_b, out_dtype=F32, bn=512, bk=d_ff // 2, residual=xp, vmem_mib=56)
        g_s = swiglu_up(rmsnorm(xs, norm_ffn[l], BF16), w1_b, w3_b)
        xs = matmul(g_s, w2_b, out_dtype=F32, bn=512, bk=d_ff // 2, residual=xs, vmem_mib=56)

    y_prompt = rmsnorm(xp, norm_final, F32).reshape(bsz, seq, d_model)
    y_sample = rmsnorm(xs, norm_final, F32).reshape(dec_b, 1, d_model)
    return (y_prompt, y_sample,
            jnp.stack(pool_p), jnp.stack(pool_s),
            jnp.stack(re_p), jnp.stack(im_p), jnp.stack(re_s), jnp.stack(im_s),
            jnp.stack(lat_p), jnp.stack(kr_p), jnp.stack(lat_s), jnp.stack(kr_s),
            jnp.stack(mk_p), jnp.stack(mv_p))
```
